```python
import math
import jax, jax.numpy as jnp
from jax import lax
import numpy as np

D_MODEL = 2048
BATCH = 8
SEQ = 2048
DEPTH = 1
DEC_BATCH = 8
DEC_SEQ = 16
PAST_LEN = 2048

CHUNK = 64
D_POOL = D_MODEL // 4
POOL_WINDOWS = (2, 4, 8, 16)
N_POOL_GROUPS = len(POOL_WINDOWS)
POOL_GROUP = D_POOL // N_POOL_GROUPS
POOL_HIST = max(POOL_WINDOWS) - 1
D_ATTN = D_MODEL - D_POOL
HEAD_DIM = 64
V_DIM = 2 * HEAD_DIM
N_HEADS = D_ATTN // V_DIM
ROT_DIM = HEAD_DIM // 4
ROPE_THETA = 500000.0
Q_BLOCK = 128
D_IN = D_POOL + 3 * D_ATTN
PEER_HEADS = 8
N_KEYS = 128
N_EXPERTS = N_KEYS * N_KEYS
D_KEY = 256
D_HALF = D_KEY // 2
PEER_TOPK = 16
PEER_BLOCK = 128
EPS = 1e-6

kernel_name = "hybrid_pool_diffattn_peer_stream_step"


def rmsnorm(x, g):
    xf = x.astype(jnp.float32)
    r = lax.rsqrt(jnp.mean(xf * xf, axis=-1, keepdims=True) + EPS)
    return (xf * r * g.astype(jnp.float32)).astype(x.dtype)


def rope_partial(x, pos):
    inv = ROPE_THETA ** (-jnp.arange(0, ROT_DIM, 2, dtype=jnp.float32) / ROT_DIM)
    ang = pos.astype(jnp.float32)[:, None] * inv[None, :]
    cos = jnp.cos(ang)[None, :, None, None, :]
    sin = jnp.sin(ang)[None, :, None, None, :]
    xf = x.astype(jnp.float32)
    half = ROT_DIM // 2
    x1 = xf[..., :half]
    x2 = xf[..., half:ROT_DIM]
    out = jnp.concatenate([x1 * cos - x2 * sin, x2 * cos + x1 * sin, xf[..., ROT_DIM:]], axis=-1)
    return out.astype(x.dtype)


def pool_mixer(u_hist, u_new, pos0, w_pool, pool_scale):
    B, S, _ = u_new.shape
    u = jnp.concatenate([u_hist, u_new], axis=1).astype(jnp.float32)
    cs = jnp.concatenate([jnp.zeros((B, 1, D_POOL), jnp.float32), jnp.cumsum(u, axis=1)], axis=1)
    pos = pos0 + jnp.arange(S)
    end = POOL_HIST + 1
    outs = []
    for g, w in enumerate(POOL_WINDOWS):
        lo, hi = g * POOL_GROUP, (g + 1) * POOL_GROUP
        win_sum = cs[:, end:end + S, lo:hi] - cs[:, end - w:end - w + S, lo:hi]
        cnt = jnp.minimum(pos + 1, w).astype(jnp.float32)[None, :, None]
        d = win_sum / cnt - u_new[..., lo:hi].astype(jnp.float32)
        outs.append(jnp.einsum('bsc,cd->bsd', d, w_pool[g].astype(jnp.float32)))
    out = jnp.concatenate(outs, axis=-1) * pool_scale.astype(jnp.float32)
    new_hist = u[:, -POOL_HIST:].astype(u_new.dtype)
    return out.astype(u_new.dtype), new_hist


def diff_attend(q, q_pos, k, v, k_pos, lam):
    s = jnp.einsum('bqhcd,bkhcd->bhcqk', q, k).astype(jnp.float32) * (HEAD_DIM ** -0.5)
    mask = (k_pos[None, :] // CHUNK) <= (q_pos[:, None] // CHUNK)
    s = jnp.where(mask[None, None, None], s, jnp.finfo(jnp.float32).min)
    p = jax.nn.softmax(s, axis=-1)
    w = p[:, :, 0] - lam * p[:, :, 1]
    return jnp.einsum('bhqk,bkhe->bqhe', w.astype(v.dtype), v)


def peer_block(xb, w_pq, sub_keys, u_tab, v_tab):
    T = xb.shape[0]
    q = (xb @ w_pq).reshape(T, PEER_HEADS, 2, D_HALF)
    s = jnp.einsum('thcd,ckd->thck', q, sub_keys).astype(jnp.float32)
    s1, i1 = lax.top_k(s[:, :, 0], PEER_TOPK)
    s2, i2 = lax.top_k(s[:, :, 1], PEER_TOPK)
    cand = (s1[..., :, None] + s2[..., None, :]).reshape(T, PEER_HEADS, PEER_TOPK * PEER_TOPK)
    cidx = (i1[..., :, None] * N_KEYS + i2[..., None, :]).reshape(T, PEER_HEADS, PEER_TOPK * PEER_TOPK)
    best, sel = lax.top_k(cand, PEER_TOPK)
    eidx = jnp.take_along_axis(cidx, sel, axis=-1)
    g = jax.nn.softmax(best, axis=-1)
    u = jnp.take(u_tab, eidx, axis=0)
    a = jax.nn.gelu(jnp.einsum('thkd,td->thk', u, xb).astype(jnp.float32))
    vv = jnp.take(v_tab, eidx, axis=0)
    return jnp.einsum('thk,thkd->td', (g * a).astype(xb.dtype), vv)


def peer(xn, w_pq, sub_keys, u_tab, v_tab):
    B, S, D = xn.shape
    T = B * S
    n_blk = -(-T // PEER_BLOCK)
    xt = jnp.pad(xn.reshape(T, D), ((0, n_blk * PEER_BLOCK - T), (0, 0)))
    out = lax.map(lambda xb: peer_block(xb, w_pq, sub_keys, u_tab, v_tab),
                  xt.reshape(n_blk, PEER_BLOCK, D))
    return out.reshape(n_blk * PEER_BLOCK, D)[:T].reshape(B, S, D)


def layer(x, k_hist, v_hist, pool_hist, norm1_g, w_in, w_pool, pool_scale, q_norm_g, k_norm_g,
          lambda_q1, lambda_k1, lambda_q2, lambda_k2, subln_g, w_out, norm2_g, w_pq, sub_keys,
          u_tab, v_tab, lam_init):
    B, S, _ = x.shape
    pos0 = k_hist.shape[1]
    pos = pos0 + jnp.arange(S)
    xn = rmsnorm(x, norm1_g)
    proj = xn @ w_in
    u_pool = proj[..., :D_POOL]
    q = proj[..., D_POOL:D_POOL + D_ATTN].reshape(B, S, N_HEADS, 2, HEAD_DIM)
    k = proj[..., D_POOL + D_ATTN:D_POOL + 2 * D_ATTN].reshape(B, S, N_HEADS, 2, HEAD_DIM)
    v = proj[..., D_POOL + 2 * D_ATTN:].reshape(B, S, N_HEADS, V_DIM)
    q = rope_partial(rmsnorm(q, q_norm_g), pos)
    k = rope_partial(rmsnorm(k, k_norm_g), pos)
    pool_out, pool_new = pool_mixer(pool_hist, u_pool, pos0, w_pool, pool_scale)
    k_all = jnp.concatenate([k_hist, k], axis=1)
    v_all = jnp.concatenate([v_hist, v], axis=1)
    k_pos = jnp.arange(pos0 + S)
    lam = (jnp.exp(jnp.sum(lambda_q1.astype(jnp.float32) * lambda_k1.astype(jnp.float32)))
           - jnp.exp(jnp.sum(lambda_q2.astype(jnp.float32) * lambda_k2.astype(jnp.float32)))
           + lam_init)
    if S % Q_BLOCK == 0 and S > Q_BLOCK:
        n_blk = S // Q_BLOCK
        qb = jnp.moveaxis(q.reshape(B, n_blk, Q_BLOCK, N_HEADS, 2, HEAD_DIM), 1, 0)
        pb = pos.reshape(n_blk, Q_BLOCK)
        o = lax.map(lambda a: diff_attend(a[0], a[1], k_all, v_all, k_pos, lam), (qb, pb))
        o = jnp.moveaxis(o, 0, 1).reshape(B, S, N_HEADS, V_DIM)
    else:
        o = diff_attend(q, pos, k_all, v_all, k_pos, lam)
    o = (rmsnorm(o, subln_g).astype(jnp.float32) * (1.0 - lam_init)).astype(x.dtype)
    mix = jnp.concatenate([pool_out, o.reshape(B, S, D_ATTN)], axis=-1) @ w_out
    h = x + mix
    y = h + peer(rmsnorm(h, norm2_g), w_pq, sub_keys, u_tab, v_tab)
    return y, k, v, pool_new


def setup_inputs(seed: int = 0) -> dict:
    key = jax.random.key(seed)
    ks = jax.random.split(key, 24)
    f32 = jnp.float32
    nrm = lambda k, shp: jax.random.normal(k, shp, f32)
    return {
        "x_prompt": nrm(ks[0], (BATCH, SEQ, D_MODEL)),
        "x_sample": nrm(ks[1], (DEC_BATCH, DEC_SEQ, D_MODEL)),
        "cache_k": nrm(ks[2], (DEPTH, DEC_BATCH, PAST_LEN, N_HEADS, 2, HEAD_DIM)),
        "cache_v": nrm(ks[3], (DEPTH, DEC_BATCH, PAST_LEN, N_HEADS, V_DIM)),
        "state_pool": nrm(ks[4], (DEPTH, DEC_BATCH, POOL_HIST, D_POOL)),
        "norm1_g": 1.0 + 0.02 * nrm(ks[5], (DEPTH, D_MODEL)),
        "w_in": nrm(ks[6], (DEPTH, D_MODEL, D_IN)) * D_MODEL ** -0.5,
        "w_pool": nrm(ks[7], (DEPTH, N_POOL_GROUPS, POOL_GROUP, POOL_GROUP)) * POOL_GROUP ** -0.5,
        "pool_scale": 1.0 + 0.02 * nrm(ks[8], (DEPTH, D_POOL)),
        "q_norm_g": 1.0 + 0.02 * nrm(ks[9], (DEPTH, HEAD_DIM)),
        "k_norm_g": 1.0 + 0.02 * nrm(ks[10], (DEPTH, HEAD_DIM)),
        "lambda_q1": 0.1 * nrm(ks[11], (DEPTH, HEAD_DIM)),
        "lambda_k1": 0.1 * nrm(ks[12], (DEPTH, HEAD_DIM)),
        "lambda_q2": 0.1 * nrm(ks[13], (DEPTH, HEAD_DIM)),
        "lambda_k2": 0.1 * nrm(ks[14], (DEPTH, HEAD_DIM)),
        "subln_g": 1.0 + 0.02 * nrm(ks[15], (DEPTH, V_DIM)),
        "w_out": nrm(ks[16], (DEPTH, D_MODEL, D_MODEL)) * D_MODEL ** -0.5,
        "norm2_g": 1.0 + 0.02 * nrm(ks[17], (DEPTH, D_MODEL)),
        "w_pq": nrm(ks[18], (DEPTH, D_MODEL, PEER_HEADS * D_KEY)) * D_MODEL ** -0.5,
        "sub_keys": nrm(ks[19], (DEPTH, 2, N_KEYS, D_HALF)) * D_HALF ** -0.5,
        "u_tab": nrm(ks[20], (DEPTH, N_EXPERTS, D_MODEL)) * D_MODEL ** -0.5,
        "v_tab": nrm(ks[21], (DEPTH, N_EXPERTS, D_MODEL)) * PEER_HEADS ** -0.5,
    }


def reference(x_prompt, x_sample, cache_k, cache_v, state_pool, norm1_g, w_in, w_pool, pool_scale,
              q_norm_g, k_norm_g, lambda_q1, lambda_k1, lambda_q2, lambda_k2, subln_g, w_out,
              norm2_g, w_pq, sub_keys, u_tab, v_tab):
    B = x_prompt.shape[0]
    yp, ys = x_prompt, x_sample
    kp_l, vp_l, pp_l, ks_l, vs_l, ps_l = [], [], [], [], [], []
    for l in range(DEPTH):
        lam_init = 0.8 - 0.6 * math.exp(-0.3 * (l + 1))
        params = (norm1_g[l], w_in[l], w_pool[l], pool_scale[l], q_norm_g[l], k_norm_g[l],
                  lambda_q1[l], lambda_k1[l], lambda_q2[l], lambda_k2[l], subln_g[l], w_out[l],
                  norm2_g[l], w_pq[l], sub_keys[l], u_tab[l], v_tab[l])
        k0 = jnp.zeros((B, 0, N_HEADS, 2, HEAD_DIM), yp.dtype)
        v0 = jnp.zeros((B, 0, N_HEADS, V_DIM), yp.dtype)
        p0 = jnp.zeros((B, POOL_HIST, D_POOL), yp.dtype)
        yp, kp, vp, pp = layer(yp, k0, v0, p0, *params, lam_init)
        ys, ks_, vs_, ps_ = layer(ys, cache_k[l], cache_v[l], state_pool[l], *params, lam_init)
        kp_l.append(kp); vp_l.append(vp); pp_l.append(pp)
        ks_l.append(ks_); vs_l.append(vs_); ps_l.append(ps_)
    return (yp, ys, jnp.stack(kp_l), jnp.stack(vp_l), jnp.stack(pp_l),
            jnp.stack(ks_l), jnp.stack(vs_l), jnp.stack(ps_l))
```

```python
import functools
import math

import jax
import jax.numpy as jnp
from jax import lax
from jax.experimental import pallas as pl
from jax.experimental.pallas import tpu as pltpu

F32 = jnp.float32
BF16 = jnp.bfloat16

CHUNK = 64
POOL_WINDOWS = (2, 4, 8, 16)
POOL_HIST = max(POOL_WINDOWS) - 1
HEAD_DIM = 64
V_DIM = 2 * HEAD_DIM
ROT_DIM = HEAD_DIM // 4
ROT_HALF = ROT_DIM // 2
ROPE_THETA = 500000.0
PEER_HEADS = 8
N_KEYS = 128
PEER_TOPK = 16
EPS = 1e-6
NEG = -1e30
LANES = 128
VMEM_LIMIT = 56 * 1024 * 1024


def _cparams(*sem):
    return pltpu.CompilerParams(dimension_semantics=sem, vmem_limit_bytes=VMEM_LIMIT)


def _nt_dot(a, b):
    return lax.dot_general(a, b, (((1,), (1,)), ((), ())), preferred_element_type=F32)


def _in_proj_body(x_ref, g1_ref, wu_ref, wq_ref, wk_ref, wv_ref, gq_ref, gk_ref,
                  cos_ref, sa_ref, sb_ref, gm_ref,
                  u_ref, q_ref, kf_ref, kb_ref, vf_ref, vb_ref, xn_ref, *, q_scale):
    @pl.when(pl.program_id(1) == 0)
    def _():
        x = x_ref[...]
        r = lax.rsqrt(jnp.mean(x * x, axis=-1, keepdims=True) + EPS)
        xn = (x * r * g1_ref[...]).astype(BF16)
        xn_ref[...] = xn
        u_ref[...] = jnp.dot(xn, wu_ref[...], preferred_element_type=F32)

    xn = xn_ref[...]
    width = q_ref.shape[-1]

    def norm_rope(y, g_ref):
        ms = jnp.dot((y * y).astype(BF16), gm_ref[...], preferred_element_type=F32)
        yn = y * lax.rsqrt(ms + EPS) * g_ref[...]
        return (yn * cos_ref[...] + pltpu.roll(yn, width - ROT_HALF, 1) * sa_ref[...]
                + pltpu.roll(yn, ROT_HALF, 1) * sb_ref[...])

    q = norm_rope(jnp.dot(xn, wq_ref[...], preferred_element_type=F32), gq_ref)
    q_ref[...] = (q * q_scale).astype(BF16)
    k = norm_rope(jnp.dot(xn, wk_ref[...], preferred_element_type=F32), gk_ref)
    kf_ref[...] = k
    kb_ref[...] = k.astype(BF16)
    v = jnp.dot(xn, wv_ref[...], preferred_element_type=F32)
    vf_ref[...] = v
    vb_ref[...] = v.astype(BF16)


def _rope_tables(pos, width):
    n = pos.shape[0]
    inv = ROPE_THETA ** (-jnp.arange(0, ROT_DIM, 2, dtype=F32) / ROT_DIM)
    ang = pos.astype(F32)[:, None] * inv[None, :]
    cos, sin = jnp.cos(ang), jnp.sin(ang)
    zero_h = jnp.zeros((n, ROT_HALF), F32)
    rest0 = jnp.zeros((n, HEAD_DIM - ROT_DIM), F32)
    c64 = jnp.concatenate([cos, cos, jnp.ones((n, HEAD_DIM - ROT_DIM), F32)], axis=-1)
    sa64 = jnp.concatenate([-sin, zero_h, rest0], axis=-1)
    sb64 = jnp.concatenate([zero_h, sin, rest0], axis=-1)
    reps = width // HEAD_DIM
    return tuple(jnp.tile(t, (1, reps)) for t in (c64, sa64, sb64))


def _in_proj(x2, seq, pos0, g1, w_in_b, gq, gk, d_pool, d_attn, tm):
    t_tok, d_model = x2.shape
    tn = d_pool
    assert d_attn % tn == 0 and t_tok % tm == 0 and tn % HEAD_DIM == 0
    nj = d_attn // tn
    if seq >= tm:
        assert seq % tm == 0
        nrep = seq // tm
        pos = pos0 + jnp.arange(seq)
    else:
        assert tm % seq == 0
        nrep = 1
        pos = pos0 + (jnp.arange(tm) % seq)
    cos_t, sa_t, sb_t = _rope_tables(pos, tn)
    gq_t = jnp.tile(gq.astype(F32)[None, :], (1, tn // HEAD_DIM))
    gk_t = jnp.tile(gk.astype(F32)[None, :], (1, tn // HEAD_DIM))
    grp = jnp.arange(tn) // HEAD_DIM
    gmat = jnp.where(grp[:, None] == grp[None, :], 1.0 / HEAD_DIM, 0.0).astype(BF16)
    q_scale = HEAD_DIM ** -0.5 * math.log2(math.e)

    row = lambda i, j: (i, 0)
    const = lambda i, j: (0, 0)
    wspec = lambda off: pl.BlockSpec((d_model, tn), lambda i, j: (0, off + j))
    tab = pl.BlockSpec((tm, tn), lambda i, j: (i % nrep, 0))
    oblk = pl.BlockSpec((tm, tn), lambda i, j: (i, j))
    outs = pl.pallas_call(
        functools.partial(_in_proj_body, q_scale=q_scale),
        grid=(t_tok // tm, nj),
        in_specs=[pl.BlockSpec((tm, d_model), row), pl.BlockSpec((1, d_model), const),
                  pl.BlockSpec((d_model, tn), const), wspec(1), wspec(1 + nj), wspec(1 + 2 * nj),
                  pl.BlockSpec((1, tn), const), pl.BlockSpec((1, tn), const),
                  tab, tab, tab, pl.BlockSpec((tn, tn), const)],
        out_specs=[pl.BlockSpec((tm, tn), row), oblk, oblk, oblk, oblk, oblk],
        out_shape=[jax.ShapeDtypeStruct((t_tok, d_pool), F32),
                   jax.ShapeDtypeStruct((t_tok, d_attn), BF16),
                   jax.ShapeDtypeStruct((t_tok, d_attn), F32),
                   jax.ShapeDtypeStruct((t_tok, d_attn), BF16),
                   jax.ShapeDtypeStruct((t_tok, d_attn), F32),
                   jax.ShapeDtypeStruct((t_tok, d_attn), BF16)],
        scratch_shapes=[pltpu.VMEM((tm, d_model), BF16)],
        compiler_params=_cparams("parallel", "arbitrary"),
        name="in_proj",
    )(x2, g1.astype(F32)[None, :], w_in_b, w_in_b, w_in_b, w_in_b, gq_t, gk_t, cos_t, sa_t, sb_t, gmat)
    return outs


def _pool_body(hist_ref, u_ref, prev_ref, wp_ref, ps_ref, o_ref, ext_ref, *, rc, pos0):
    r = pl.program_id(1)
    halo = jnp.where(r == 0, hist_ref[...], prev_ref[...])
    hrows = halo.shape[0]
    ext_ref[0:hrows, :] = halo
    ext_ref[hrows:hrows + rc, :] = u_ref[...]
    pos = pos0 + r * rc + lax.broadcasted_iota(jnp.int32, (rc, 1), 0)
    grp = ps_ref.shape[-1] // len(POOL_WINDOWS)
    for g, w in enumerate(POOL_WINDOWS):
        lo, hi = g * grp, (g + 1) * grp
        cur = ext_ref[hrows:hrows + rc, lo:hi]
        acc = cur
        for k in range(1, w):
            acc = acc + ext_ref[hrows - k:hrows - k + rc, lo:hi]
        cnt = jnp.minimum(pos + 1, w).astype(F32)
        d = acc / cnt - cur
        out = jnp.dot(d.astype(BF16), wp_ref[g], preferred_element_type=F32) * ps_ref[:, lo:hi]
        o_ref[:, lo:hi] = out.astype(BF16)


def _pool(u3, hist, pos0, w_pool_b, pool_scale):
    b, s, dp = u3.shape
    assert s >= POOL_HIST and s % 16 == 0
    rc = min(s, 512)
    assert s % rc == 0
    hist16 = jnp.pad(hist.astype(F32), ((0, 0), (16 - POOL_HIST, 0), (0, 0)))
    return pl.pallas_call(
        functools.partial(_pool_body, rc=rc, pos0=pos0),
        grid=(b, s // rc),
        in_specs=[pl.BlockSpec((None, 16, dp), lambda i, r: (i, 0, 0)),
                  pl.BlockSpec((None, rc, dp), lambda i, r: (i, r, 0)),
                  pl.BlockSpec((None, 16, dp), lambda i, r: (i, jnp.maximum(r * (rc // 16) - 1, 0), 0)),
                  pl.BlockSpec(w_pool_b.shape, lambda i, r: (0, 0, 0)),
                  pl.BlockSpec((1, dp), lambda i, r: (0, 0))],
        out_specs=pl.BlockSpec((None, rc, dp), lambda i, r: (i, r, 0)),
        out_shape=jax.ShapeDtypeStruct((b, s, dp), BF16),
        scratch_shapes=[pltpu.VMEM((16 + rc, dp), F32)],
        compiler_params=_cparams("parallel", "parallel"),
        name="pool_mixer",
    )(hist16, u3, u3, w_pool_b, pool_scale.astype(F32)[None, :])


def _lambda(lq1_ref, lk1_ref, lq2_ref, lk2_ref, lam_init):
    a = jnp.sum(lq1_ref[...] * lk1_ref[...], axis=-1, keepdims=True)
    b = jnp.sum(lq2_ref[...] * lk2_ref[...], axis=-1, keepdims=True)
    return jnp.exp(a) - jnp.exp(b) + lam_init


def _split_maps(q):
    lane = lax.broadcasted_iota(jnp.int32, q.shape, 1)
    zero = jnp.zeros_like(q)
    return jnp.where(lane < HEAD_DIM, q, zero), jnp.where(lane >= HEAD_DIM, q, zero)


def _subln(o, sg_ref, lam_init):
    r = lax.rsqrt(jnp.mean(o * o, axis=-1, keepdims=True) + EPS)
    return (o * r * sg_ref[...]) * (1.0 - lam_init)


def _attn_prompt_body(lq1_ref, lk1_ref, lq2_ref, lk2_ref, sg_ref, q_ref, k_ref, v_ref, o_ref, *, tq, lam_init):
    i = pl.program_id(2)
    lam = _lambda(lq1_ref, lk1_ref, lq2_ref, lk2_ref, lam_init)
    q0, q1 = _split_maps(q_ref[...])
    q_chunk = (i * tq + lax.broadcasted_iota(jnp.int32, (tq, 1), 0)) // CHUNK

    def body(j, carry):
        off = pl.multiple_of(j * tq, tq)
        ks = k_ref[pl.ds(off, tq), :]
        vs = v_ref[pl.ds(off, tq), :]
        k_chunk = (j * tq + lax.broadcasted_iota(jnp.int32, (1, tq), 1)) // CHUNK
        ok = k_chunk <= q_chunk

        def one(qz, m, l, acc):
            s = jnp.where(ok, _nt_dot(qz, ks), NEG)
            m_new = jnp.maximum(m, jnp.max(s, axis=-1, keepdims=True))
            alpha = jnp.exp2(m - m_new)
            p = jnp.exp2(s - m_new)
            l = alpha * l + jnp.sum(p, axis=-1, keepdims=True)
            acc = alpha * acc + jnp.dot(p.astype(BF16), vs, preferred_element_type=F32)
            return m_new, l, acc

        m0, l0, a0, m1, l1, a1 = carry
        return one(q0, m0, l0, a0) + one(q1, m1, l1, a1)

    m_init = jnp.full((tq, 1), NEG, F32)
    l_init = jnp.zeros((tq, 1), F32)
    a_init = jnp.zeros((tq, V_DIM), F32)
    m0, l0, a0, m1, l1, a1 = lax.fori_loop(0, i + 1, body, (m_init, l_init, a_init) * 2)
    o = a0 / l0 - lam * (a1 / l1)
    o_ref[...] = _subln(o, sg_ref, lam_init).astype(BF16)


def _attn_sample_body(lq1_ref, lk1_ref, lq2_ref, lk2_ref, sg_ref, q_ref, ck_ref, cv_ref, kn_ref, vn_ref, o_ref,
                      *, past, lam_init):
    lam = _lambda(lq1_ref, lk1_ref, lq2_ref, lk2_ref, lam_init)
    q0, q1 = _split_maps(q_ref[...])
    sq = q_ref.shape[0]
    kc = ck_ref[...].astype(BF16)
    vc = cv_ref[...].astype(BF16)
    kn = kn_ref[...]
    vn = vn_ref[...]
    q_chunk = (past + lax.broadcasted_iota(jnp.int32, (sq, 1), 0)) // CHUNK
    k_chunk = (past + lax.broadcasted_iota(jnp.int32, (1, sq), 1)) // CHUNK
    ok_new = k_chunk <= q_chunk

    def one(qz):
        sc = _nt_dot(qz, kc)
        sn = jnp.where(ok_new, _nt_dot(qz, kn), NEG)
        m = jnp.maximum(jnp.max(sc, axis=-1, keepdims=True), jnp.max(sn, axis=-1, keepdims=True))
        pc = jnp.exp2(sc - m)
        pn = jnp.exp2(sn - m)
        l = jnp.sum(pc, axis=-1, keepdims=True) + jnp.sum(pn, axis=-1, keepdims=True)
        acc = (jnp.dot(pc.astype(BF16), vc, preferred_element_type=F32)
               + jnp.dot(pn.astype(BF16), vn, preferred_element_type=F32))
        return acc / l

    o = one(q0) - lam * one(q1)
    o_ref[...] = _subln(o, sg_ref, lam_init).astype(BF16)


def _lam_specs(nargs):
    zero = lambda *a: (0, 0)
    return [pl.BlockSpec((1, HEAD_DIM), zero)] * 4 + [pl.BlockSpec((1, V_DIM), zero)]


def _lam_args(lq1, lk1, lq2, lk2, subln_g):
    return [a.astype(F32)[None, :] for a in (lq1, lk1, lq2, lk2, subln_g)]


def _attn_prompt(q3, k3, v3, lam_args, lam_init, tq):
    b, s, da = q3.shape
    nh = da // V_DIM
    assert s % tq == 0 and tq % CHUNK == 0
    return pl.pallas_call(
        functools.partial(_attn_prompt_body, tq=tq, lam_init=lam_init),
        grid=(b, nh, s // tq),
        in_specs=_lam_specs(3) + [
            pl.BlockSpec((None, tq, V_DIM), lambda bi, h, i: (bi, i, h)),
            pl.BlockSpec((None, s, V_DIM), lambda bi, h, i: (bi, 0, h)),
            pl.BlockSpec((None, s, V_DIM), lambda bi, h, i: (bi, 0, h))],
        out_specs=pl.BlockSpec((None, tq, V_DIM), lambda bi, h, i: (bi, i, h)),
        out_shape=jax.ShapeDtypeStruct((b, s, da), BF16),
        compiler_params=_cparams("parallel", "parallel", "parallel"),
        name="attn_prompt",
    )(*lam_args, q3, k3, v3)


def _attn_sample(q3, ck3, cv3, kn3, vn3, lam_args, lam_init):
    b, s, da = q3.shape
    past = ck3.shape[1]
    nh = da // V_DIM
    assert past % CHUNK == 0
    new = lambda bi, h: (bi, 0, h)
    return pl.pallas_call(
        functools.partial(_attn_sample_body, past=past, lam_init=lam_init),
        grid=(b, nh),
        in_specs=_lam_specs(2) + [
            pl.BlockSpec((None, s, V_DIM), new),
            pl.BlockSpec((None, past, V_DIM), new), pl.BlockSpec((None, past, V_DIM), new),
            pl.BlockSpec((None, s, V_DIM), new), pl.BlockSpec((None, s, V_DIM), new)],
        out_specs=pl.BlockSpec((None, s, V_DIM), new),
        out_shape=jax.ShapeDtypeStruct((b, s, da), BF16),
        compiler_params=_cparams("parallel", "parallel"),
        name="attn_sample",
    )(*lam_args, q3, ck3, cv3, kn3, vn3)


def _out_proj_body(po_ref, on_ref, x_ref, wo_ref, g2_ref, wpq_ref, h_ref, xnt_ref, qp_ref):
    dp = po_ref.shape[-1]
    mix = (jnp.dot(po_ref[...], wo_ref[0:dp, :], preferred_element_type=F32)
           + jnp.dot(on_ref[...], wo_ref[dp:, :], preferred_element_type=F32))
    h = x_ref[...] + mix
    h_ref[...] = h
    r = lax.rsqrt(jnp.mean(h * h, axis=-1, keepdims=True) + EPS)
    xn = h * r * g2_ref[...]
    xnt_ref[...] = xn.T.astype(BF16)
    qp_ref[...] = jnp.dot(xn.astype(BF16), wpq_ref[...], preferred_element_type=F32).astype(BF16)


def _out_proj(po2, on2, x2, w_out_b, g2, w_pq_b, tm):
    t_tok, d_model = x2.shape
    dp, da = po2.shape[1], on2.shape[1]
    dq = w_pq_b.shape[1]
    row = lambda i: (i, 0)
    const = lambda i: (0, 0)
    resident = functools.partial(pl.BlockSpec, index_map=const, pipeline_mode=pl.Buffered(1))
    return pl.pallas_call(
        _out_proj_body,
        grid=(t_tok // tm,),
        in_specs=[pl.BlockSpec((tm, dp), row), pl.BlockSpec((tm, da), row), pl.BlockSpec((tm, d_model), row),
                  resident((d_model, d_model)), pl.BlockSpec((1, d_model), const), resident((d_model, dq))],
        out_specs=[pl.BlockSpec((tm, d_model), row), pl.BlockSpec((d_model, tm), lambda i: (0, i)),
                   pl.BlockSpec((tm, dq), row)],
        out_shape=[jax.ShapeDtypeStruct((t_tok, d_model), F32),
                   jax.ShapeDtypeStruct((d_model, t_tok), BF16),
                   jax.ShapeDtypeStruct((t_tok, dq), BF16)],
        compiler_params=_cparams("parallel"),
        name="out_proj",
    )(po2, on2, x2, w_out_b, g2.astype(F32)[None, :], w_pq_b)


def _top_rows(s, n_take):
    rows, cols = s.shape
    ridx = lax.broadcasted_iota(jnp.int32, (rows, cols), 0)
    tidx = lax.broadcasted_iota(jnp.int32, (n_take, cols), 0)

    def body(k, carry):
        cur, rank, srt = carry
        m = jnp.max(cur, axis=0, keepdims=True)
        first = jnp.min(jnp.where(cur == m, ridx, rows), axis=0, keepdims=True)
        sel = ridx == first
        rank = jnp.where(sel, k, rank)
        cur = jnp.where(sel, -jnp.inf, cur)
        srt = jnp.where(tidx == k, m, srt)
        return cur, rank, srt

    init = (s, jnp.full((rows, cols), n_take, jnp.int32), jnp.zeros((n_take, cols), F32))
    _, rank, srt = lax.fori_loop(0, n_take, body, init, unroll=True)
    return rank, srt


def _peer_topk_body(qp_ref, sk_ref, len_ref, c1_ref, r2_ref, e2_ref):
    qp = qp_ref[...]
    s1 = _nt_dot(sk_ref[0], qp[:, :N_KEYS])
    s2 = _nt_dot(sk_ref[1], qp[:, N_KEYS:])
    rank1, srt1 = _top_rows(s1, PEER_TOPK)
    rank2, srt2 = _top_rows(s2, PEER_TOPK)
    cand = jnp.concatenate([srt1[a:a + 1, :] + srt2 for a in range(PEER_TOPK)], axis=0)
    crank, _ = _top_rows(cand, PEER_TOPK)
    chosen = crank < PEER_TOPK
    best = srt1[0:1, :] + srt2[0:1, :]
    z = jnp.sum(jnp.where(chosen, jnp.exp(cand - best), 0.0), axis=0, keepdims=True)
    chosen_f = chosen.astype(F32)
    lenrow = jnp.zeros(s1.shape, F32)
    for a in range(PEER_TOPK):
        len_a = jnp.sum(chosen_f[a * PEER_TOPK:(a + 1) * PEER_TOPK, :], axis=0, keepdims=True)
        lenrow = lenrow + jnp.where(rank1 == a, len_a, 0.0)
    len_ref[...] = lenrow
    c1_ref[...] = jnp.exp(s1 - srt1[0:1, :]) / z
    r2_ref[...] = rank2.astype(F32)
    e2_ref[...] = jnp.exp(s2 - srt2[0:1, :])


def _peer_topk(qp2, sub_keys_b, tme):
    t_tok, dq = qp2.shape
    assert dq == PEER_HEADS * 2 * N_KEYS and t_tok % tme == 0
    out = jax.ShapeDtypeStruct((PEER_HEADS, N_KEYS, t_tok), F32)
    oblk = pl.BlockSpec((None, N_KEYS, tme), lambda i, h: (h, 0, i))
    return pl.pallas_call(
        _peer_topk_body,
        grid=(t_tok // tme, PEER_HEADS),
        in_specs=[pl.BlockSpec((tme, 2 * N_KEYS), lambda i, h: (i, h)),
                  pl.BlockSpec(sub_keys_b.shape, lambda i, h: (0, 0, 0))],
        out_specs=[oblk, oblk, oblk, oblk],
        out_shape=[out, out, out, out],
        compiler_params=_cparams("parallel", "parallel"),
        name="peer_topk",
    )(qp2, sub_keys_b)


def _peer_main_body(xnt_ref, u_ref, vt_ref, len_ref, c1_ref, r2_ref, e2_ref, out_ref, ht_ref, w_ref, *, a_per):
    @pl.when(pl.program_id(1) == 0)
    def _():
        out_ref[...] = jnp.zeros_like(out_ref)

    tm = xnt_ref.shape[-1]
    ht_ref[...] = jnp.dot(u_ref[...], xnt_ref[...], preferred_element_type=F32)

    def lane_body(lc, carry):
        ls = pl.ds(pl.multiple_of(lc * LANES, LANES), LANES)
        for al in range(a_per):
            rows = slice(al * N_KEYS, (al + 1) * N_KEYS)
            gate = jnp.zeros((N_KEYS, LANES), F32)
            for h in range(PEER_HEADS):
                ln = len_ref[h, al:al + 1, ls]
                cc = c1_ref[h, al:al + 1, ls]
                gate = gate + jnp.where(r2_ref[h, :, ls] < ln, e2_ref[h, :, ls], 0.0) * cc
            act = jax.nn.gelu(ht_ref[rows, ls])
            w_ref[rows, ls] = (gate * act).astype(BF16)
        return carry

    lax.fori_loop(0, tm // LANES, lane_body, 0)
    out_ref[...] += jnp.dot(vt_ref[...], w_ref[...], preferred_element_type=F32)


def _peer_main(xnt, u_b, vt_b, tabs, tm, ec):
    d_model, t_tok = xnt.shape
    n_exp = u_b.shape[0]
    assert n_exp == N_KEYS * N_KEYS and n_exp % ec == 0 and ec % N_KEYS == 0 and t_tok % tm == 0
    a_per = ec // N_KEYS
    row_tab = pl.BlockSpec((PEER_HEADS, a_per, tm), lambda i, c: (0, c, i))
    full_tab = pl.BlockSpec((PEER_HEADS, N_KEYS, tm), lambda i, c: (0, 0, i))
    return pl.pallas_call(
        functools.partial(_peer_main_body, a_per=a_per),
        grid=(t_tok // tm, n_exp // ec),
        in_specs=[pl.BlockSpec((d_model, tm), lambda i, c: (0, i)),
                  pl.BlockSpec((ec, d_model), lambda i, c: (c, 0)),
                  pl.BlockSpec((d_model, ec), lambda i, c: (0, c)),
                  row_tab, row_tab, full_tab, full_tab],
        out_specs=pl.BlockSpec((d_model, tm), lambda i, c: (0, i)),
        out_shape=jax.ShapeDtypeStruct((d_model, t_tok), F32),
        scratch_shapes=[pltpu.VMEM((ec, tm), F32), pltpu.VMEM((ec, tm), BF16)],
        compiler_params=_cparams("parallel", "arbitrary"),
        name="peer_main",
    )(xnt, u_b, vt_b, *tabs)


def _finish_body(h_ref, pt_ref, y_ref):
    y_ref[...] = h_ref[...] + pt_ref[...].T


def _finish(h2, peer_t, tm):
    t_tok, d_model = h2.shape
    return pl.pallas_call(
        _finish_body,
        grid=(t_tok // tm,),
        in_specs=[pl.BlockSpec((tm, d_model), lambda i: (i, 0)), pl.BlockSpec((d_model, tm), lambda i: (0, i))],
        out_specs=pl.BlockSpec((tm, d_model), lambda i: (i, 0)),
        out_shape=jax.ShapeDtypeStruct((t_tok, d_model), F32),
        compiler_params=_cparams("parallel"),
        name="finish",
    )(h2, peer_t)


def _layer(x, k_hist, v_hist, pool_hist, w, lam_init):
    b, s, d_model = x.shape
    d_pool = w["pool_scale"].shape[0]
    d_attn = d_model - d_pool
    t_tok = b * s
    past = 0 if k_hist is None else k_hist.shape[1]
    tm = 512 if t_tok % 512 == 0 else LANES
    assert t_tok % tm == 0

    x2 = x.reshape(t_tok, d_model)
    u, q, kf, kb, vf, vb = _in_proj(x2, s, past, w["norm1_g"], w["w_in"], w["q_norm_g"], w["k_norm_g"],
                                    d_pool, d_attn, tm)
    po = _pool(u.reshape(b, s, d_pool), pool_hist, past, w["w_pool"], w["pool_scale"])
    lam_args = _lam_args(w["lambda_q1"], w["lambda_k1"], w["lambda_q2"], w["lambda_k2"], w["subln_g"])
    q3, kb3, vb3 = (a.reshape(b, s, d_attn) for a in (q, kb, vb))
    if k_hist is None:
        on = _attn_prompt(q3, kb3, vb3, lam_args, lam_init, tq=256)
    else:
        on = _attn_sample(q3, k_hist, v_hist, kb3, vb3, lam_args, lam_init)
    h, xnt, qp = _out_proj(po.reshape(t_tok, d_pool), on.reshape(t_tok, d_attn), x2, w["w_out"], w["norm2_g"],
                           w["w_pq"], tm)
    tabs = _peer_topk(qp, w["sub_keys"], LANES)
    peer_t = _peer_main(xnt, w["u_tab"], w["v_tab_t"], tabs, tm, ec=1024)
    y = _finish(h, peer_t, tm)
    pool_new = u.reshape(b, s, d_pool)[:, s - POOL_HIST:, :]
    return y.reshape(b, s, d_model), kf, vf, pool_new


def kernel(x_prompt, x_sample, cache_k, cache_v, state_pool, norm1_g, w_in, w_pool, pool_scale, q_norm_g, k_norm_g,
           lambda_q1, lambda_k1, lambda_q2, lambda_k2, subln_g, w_out, norm2_g, w_pq, sub_keys, u_tab, v_tab):
    depth = w_in.shape[0]
    bp, sp, d_model = x_prompt.shape
    bs, ss, _ = x_sample.shape
    d_pool = pool_scale.shape[-1]
    d_attn = d_model - d_pool
    n_heads = d_attn // V_DIM
    yp, ys = x_prompt, x_sample
    outs = [[] for _ in range(6)]
    for l in range(depth):
        lam_init = 0.8 - 0.6 * math.exp(-0.3 * (l + 1))
        w = dict(norm1_g=norm1_g[l], w_in=w_in[l].astype(BF16), w_pool=w_pool[l].astype(BF16),
                 pool_scale=pool_scale[l], q_norm_g=q_norm_g[l], k_norm_g=k_norm_g[l],
                 lambda_q1=lambda_q1[l], lambda_k1=lambda_k1[l], lambda_q2=lambda_q2[l], lambda_k2=lambda_k2[l],
                 subln_g=subln_g[l], w_out=w_out[l].astype(BF16), norm2_g=norm2_g[l], w_pq=w_pq[l].astype(BF16),
                 sub_keys=sub_keys[l].astype(BF16), u_tab=u_tab[l].astype(BF16),
                 v_tab_t=v_tab[l].T.astype(BF16))
        zero_hist = jnp.zeros((bp, POOL_HIST, d_pool), x_prompt.dtype)
        yp, kp, vp, pp = _layer(yp, None, None, zero_hist, w, lam_init)
        past = cache_k.shape[2]
        ys, ks, vs, ps = _layer(ys, cache_k[l].reshape(bs, past, d_attn), cache_v[l].reshape(bs, past, d_attn),
                                state_pool[l], w, lam_init)
        for lst, val in zip(outs, (kp.reshape(bp, sp, n_heads, 2, HEAD_DIM), vp.reshape(bp, sp, n_heads, V_DIM), pp,
                                   ks.reshape(bs, ss, n_heads, 2, HEAD_DIM), vs.reshape(bs, ss, n_heads, V_DIM), ps)):
            lst.append(val)
    return (yp, ys) + tuple(jnp.stack(o) for o in outs)
```

```python
import functools
import math

import jax
import jax.numpy as jnp
from jax import lax
from jax.experimental import pallas as pl
from jax.experimental.pallas import tpu as pltpu

F32 = jnp.float32
BF16 = jnp.bfloat16

CHUNK = 64
POOL_WINDOWS = (2, 4, 8, 16)
POOL_HIST = max(POOL_WINDOWS) - 1
HEAD_DIM = 64
V_DIM = 2 * HEAD_DIM
ROT_DIM = HEAD_DIM // 4
ROT_HALF = ROT_DIM // 2
ROPE_THETA = 500000.0
PEER_HEADS = 8
N_KEYS = 128
PEER_TOPK = 16
EPS = 1e-6
NEG = -1e30
LANES = 128
VMEM_LIMIT = 56 * 1024 * 1024


def _cparams(*sem, flags=None):
    return pltpu.CompilerParams(dimension_semantics=sem, vmem_limit_bytes=VMEM_LIMIT, flags=flags)


def _nt_dot(a, b):
    return lax.dot_general(a, b, (((1,), (1,)), ((), ())), preferred_element_type=F32)


def _in_proj_body(x_ref, g1_ref, wu_ref, wq_ref, wk_ref, wv_ref, gq_ref, gk_ref,
                  cos_ref, sa_ref, sb_ref, gm_ref,
                  u_ref, q_ref, kf_ref, kb_ref, vf_ref, vb_ref, xn_ref, *, q_scale, head_major):
    @pl.when(pl.program_id(1) == 0)
    def _():
        x = x_ref[...]
        r = lax.rsqrt(jnp.mean(x * x, axis=-1, keepdims=True) + EPS)
        xn = (x * r * g1_ref[...]).astype(BF16)
        xn_ref[...] = xn
        u_ref[...] = jnp.dot(xn, wu_ref[...], preferred_element_type=F32)

    xn = xn_ref[...]
    width = q_ref.shape[-1]

    def norm_rope(y, g_ref):
        ms = jnp.dot((y * y).astype(BF16), gm_ref[...], preferred_element_type=F32)
        yn = y * lax.rsqrt(ms + EPS) * g_ref[...]
        return (yn * cos_ref[...] + pltpu.roll(yn, width - ROT_HALF, 1) * sa_ref[...]
                + pltpu.roll(yn, ROT_HALF, 1) * sb_ref[...])

    q = norm_rope(jnp.dot(xn, wq_ref[...], preferred_element_type=F32), gq_ref)
    q_ref[...] = (q * q_scale).astype(BF16)
    k = norm_rope(jnp.dot(xn, wk_ref[...], preferred_element_type=F32), gk_ref)
    v = jnp.dot(xn, wv_ref[...], preferred_element_type=F32)
    if head_major:
        kt = k.T
        kf_ref[...] = kt
        kb_ref[...] = kt.astype(BF16)
        for hh in range(width // V_DIM):
            vh = v[:, hh * V_DIM:(hh + 1) * V_DIM]
            vf_ref[hh] = vh
            vb_ref[hh] = vh.astype(BF16)
    else:
        kf_ref[...] = k
        kb_ref[...] = k.astype(BF16)
        vf_ref[...] = v
        vb_ref[...] = v.astype(BF16)


def _rope_tables(pos, width):
    n = pos.shape[0]
    inv = ROPE_THETA ** (-jnp.arange(0, ROT_DIM, 2, dtype=F32) / ROT_DIM)
    ang = pos.astype(F32)[:, None] * inv[None, :]
    cos, sin = jnp.cos(ang), jnp.sin(ang)
    zero_h = jnp.zeros((n, ROT_HALF), F32)
    rest0 = jnp.zeros((n, HEAD_DIM - ROT_DIM), F32)
    c64 = jnp.concatenate([cos, cos, jnp.ones((n, HEAD_DIM - ROT_DIM), F32)], axis=-1)
    sa64 = jnp.concatenate([-sin, zero_h, rest0], axis=-1)
    sb64 = jnp.concatenate([zero_h, sin, rest0], axis=-1)
    reps = width // HEAD_DIM
    return tuple(jnp.tile(t, (1, reps)) for t in (c64, sa64, sb64))


def _in_proj(x2, seq, pos0, g1, w_in_b, gq, gk, d_pool, d_attn, tm):
    t_tok, d_model = x2.shape
    tn = d_pool
    assert d_attn % tn == 0 and t_tok % tm == 0 and tn % V_DIM == 0
    nj = d_attn // tn
    head_major = seq % tm == 0
    if seq >= tm:
        assert seq % tm == 0
        nrep = seq // tm
        pos = pos0 + jnp.arange(seq)
    else:
        assert tm % seq == 0
        nrep = 1
        pos = pos0 + (jnp.arange(tm) % seq)
    cos_t, sa_t, sb_t = _rope_tables(pos, tn)
    gq_t = jnp.tile(gq.astype(F32)[None, :], (1, tn // HEAD_DIM))
    gk_t = jnp.tile(gk.astype(F32)[None, :], (1, tn // HEAD_DIM))
    grp = jnp.arange(tn) // HEAD_DIM
    gmat = jnp.where(grp[:, None] == grp[None, :], 1.0 / HEAD_DIM, 0.0).astype(BF16)
    q_scale = HEAD_DIM ** -0.5 * math.log2(math.e)

    row = lambda i, j: (i, 0)
    const = lambda i, j: (0, 0)
    wspec = lambda off: pl.BlockSpec((d_model, tn), lambda i, j: (0, off + j))
    tab = pl.BlockSpec((tm, tn), lambda i, j: (i % nrep, 0))
    oblk = pl.BlockSpec((tm, tn), lambda i, j: (i, j))
    if head_major:
        n_b, hpb = t_tok // seq, tn // V_DIM
        kblk = pl.BlockSpec((None, tn, tm), lambda i, j: (i // nrep, j, i % nrep))
        vblk = pl.BlockSpec((None, hpb, tm, V_DIM), lambda i, j: (i // nrep, j, i % nrep, 0))
        k_shape, v_shape = (n_b, d_attn, seq), (n_b, d_attn // V_DIM, seq, V_DIM)
    else:
        kblk = vblk = oblk
        k_shape = v_shape = (t_tok, d_attn)
    outs = pl.pallas_call(
        functools.partial(_in_proj_body, q_scale=q_scale, head_major=head_major),
        grid=(t_tok // tm, nj),
        in_specs=[pl.BlockSpec((tm, d_model), row), pl.BlockSpec((1, d_model), const),
                  pl.BlockSpec((d_model, tn), const), wspec(1), wspec(1 + nj), wspec(1 + 2 * nj),
                  pl.BlockSpec((1, tn), const), pl.BlockSpec((1, tn), const),
                  tab, tab, tab, pl.BlockSpec((tn, tn), const)],
        out_specs=[pl.BlockSpec((tm, tn), row), oblk, kblk, kblk, vblk, vblk],
        out_shape=[jax.ShapeDtypeStruct((t_tok, d_pool), F32),
                   jax.ShapeDtypeStruct((t_tok, d_attn), BF16),
                   jax.ShapeDtypeStruct(k_shape, F32),
                   jax.ShapeDtypeStruct(k_shape, BF16),
                   jax.ShapeDtypeStruct(v_shape, F32),
                   jax.ShapeDtypeStruct(v_shape, BF16)],
        scratch_shapes=[pltpu.VMEM((tm, d_model), BF16)],
        compiler_params=_cparams("parallel", "arbitrary"),
        name="in_proj",
    )(x2, g1.astype(F32)[None, :], w_in_b, w_in_b, w_in_b, w_in_b, gq_t, gk_t, cos_t, sa_t, sb_t, gmat)
    return outs


def _pool_body(hist_ref, u_ref, prev_ref, wp_ref, ps_ref, o_ref, ext_ref, *, rc, pos0):
    r = pl.program_id(1)
    halo = jnp.where(r == 0, hist_ref[...], prev_ref[...])
    hrows = halo.shape[0]
    ext_ref[0:hrows, :] = halo
    ext_ref[hrows:hrows + rc, :] = u_ref[...]
    pos = pos0 + r * rc + lax.broadcasted_iota(jnp.int32, (rc, 1), 0)
    grp = ps_ref.shape[-1] // len(POOL_WINDOWS)
    for g, w in enumerate(POOL_WINDOWS):
        lo, hi = g * grp, (g + 1) * grp
        cur = ext_ref[hrows:hrows + rc, lo:hi]
        acc = cur
        for k in range(1, w):
            acc = acc + ext_ref[hrows - k:hrows - k + rc, lo:hi]
        cnt = jnp.minimum(pos + 1, w).astype(F32)
        d = acc / cnt - cur
        out = jnp.dot(d.astype(BF16), wp_ref[g], preferred_element_type=F32) * ps_ref[:, lo:hi]
        o_ref[:, lo:hi] = out.astype(BF16)


def _pool(u3, hist, pos0, w_pool_b, pool_scale):
    b, s, dp = u3.shape
    assert s >= POOL_HIST and s % 16 == 0
    rc = min(s, 512)
    assert s % rc == 0
    hist16 = jnp.pad(hist.astype(F32), ((0, 0), (16 - POOL_HIST, 0), (0, 0)))
    return pl.pallas_call(
        functools.partial(_pool_body, rc=rc, pos0=pos0),
        grid=(b, s // rc),
        in_specs=[pl.BlockSpec((None, 16, dp), lambda i, r: (i, 0, 0)),
                  pl.BlockSpec((None, rc, dp), lambda i, r: (i, r, 0)),
                  pl.BlockSpec((None, 16, dp), lambda i, r: (i, jnp.maximum(r * (rc // 16) - 1, 0), 0)),
                  pl.BlockSpec(w_pool_b.shape, lambda i, r: (0, 0, 0)),
                  pl.BlockSpec((1, dp), lambda i, r: (0, 0))],
        out_specs=pl.BlockSpec((None, rc, dp), lambda i, r: (i, r, 0)),
        out_shape=jax.ShapeDtypeStruct((b, s, dp), BF16),
        scratch_shapes=[pltpu.VMEM((16 + rc, dp), F32)],
        compiler_params=_cparams("parallel", "parallel"),
        name="pool_mixer",
    )(hist16, u3, u3, w_pool_b, pool_scale.astype(F32)[None, :])


def _lambda(lq1_ref, lk1_ref, lq2_ref, lk2_ref, lam_init):
    a = jnp.sum(lq1_ref[...] * lk1_ref[...], axis=-1, keepdims=True)
    b = jnp.sum(lq2_ref[...] * lk2_ref[...], axis=-1, keepdims=True)
    return jnp.exp(a) - jnp.exp(b) + lam_init


def _split_maps(q):
    lane = lax.broadcasted_iota(jnp.int32, q.shape, 1)
    zero = jnp.zeros_like(q)
    return jnp.where(lane < HEAD_DIM, q, zero), jnp.where(lane >= HEAD_DIM, q, zero)


def _subln(o, sg_ref, lam_init):
    r = lax.rsqrt(jnp.mean(o * o, axis=-1, keepdims=True) + EPS)
    return (o * r * sg_ref[...]) * (1.0 - lam_init)


def _attn_prompt_body(lq1_ref, lk1_ref, lq2_ref, lk2_ref, sg_ref, q_ref, kt_ref, v_ref, o_ref,
                      *, tq, g_heads, lam_init):
    i = pl.program_id(2)
    lam = _lambda(lq1_ref, lk1_ref, lq2_ref, lk2_ref, lam_init)
    head = lambda g: slice(g * V_DIM, (g + 1) * V_DIM)
    qz = []
    for g in range(g_heads):
        qz += list(_split_maps(q_ref[:, head(g)]))

    ones = jnp.ones((tq, V_DIM), BF16)

    def step(j, carry, bias):
        off = pl.multiple_of(j * tq, tq)
        out = []
        for g in range(g_heads):
            kt = kt_ref[head(g), pl.ds(off, tq)]
            v1 = jnp.concatenate([v_ref[g, pl.ds(off, tq), :], ones], axis=-1)
            for c in range(2):
                m, acc = carry[2 * (2 * g + c):2 * (2 * g + c) + 2]
                s = jnp.dot(qz[2 * g + c], kt, preferred_element_type=F32)
                if bias is not None:
                    s = s + bias
                m_new = jnp.maximum(m, jnp.max(s, axis=-1, keepdims=True))
                p = jnp.exp2(s - m_new)
                acc = jnp.exp2(m - m_new) * acc + jnp.dot(p.astype(BF16), v1, preferred_element_type=F32)
                out += [m_new, acc]
        return tuple(out)

    init = (jnp.full((tq, 1), NEG, F32), jnp.zeros((tq, 2 * V_DIM), F32)) * (2 * g_heads)
    carry = lax.fori_loop(0, i, lambda j, c: step(j, c, None), init)
    row_chunk = lax.broadcasted_iota(jnp.int32, (tq, tq), 0) // CHUNK
    col_chunk = lax.broadcasted_iota(jnp.int32, (tq, tq), 1) // CHUNK
    carry = step(i, carry, jnp.where(col_chunk <= row_chunk, 0.0, NEG))
    for g in range(g_heads):
        _, a0, _, a1 = carry[4 * g:4 * g + 4]
        o = a0[:, :V_DIM] / a0[:, V_DIM:] - lam * (a1[:, :V_DIM] / a1[:, V_DIM:])
        o_ref[:, head(g)] = _subln(o, sg_ref, lam_init).astype(BF16)


def _attn_sample_body(lq1_ref, lk1_ref, lq2_ref, lk2_ref, sg_ref, q_ref, ck_ref, cv_ref, kn_ref, vn_ref, o_ref,
                      *, past, lam_init):
    lam = _lambda(lq1_ref, lk1_ref, lq2_ref, lk2_ref, lam_init)
    q0, q1 = _split_maps(q_ref[...])
    sq = q_ref.shape[0]
    kc = ck_ref[...].astype(BF16)
    vc = cv_ref[...].astype(BF16)
    kn = kn_ref[...]
    vn = vn_ref[...]
    q_chunk = (past + lax.broadcasted_iota(jnp.int32, (sq, 1), 0)) // CHUNK
    k_chunk = (past + lax.broadcasted_iota(jnp.int32, (1, sq), 1)) // CHUNK
    ok_new = k_chunk <= q_chunk

    def one(qz):
        sc = jnp.dot(qz, kc, preferred_element_type=F32)
        sn = jnp.where(ok_new, _nt_dot(qz, kn), NEG)
        m = jnp.maximum(jnp.max(sc, axis=-1, keepdims=True), jnp.max(sn, axis=-1, keepdims=True))
        pc = jnp.exp2(sc - m)
        pn = jnp.exp2(sn - m)
        l = jnp.sum(pc, axis=-1, keepdims=True) + jnp.sum(pn, axis=-1, keepdims=True)
        acc = (jnp.dot(pc.astype(BF16), vc, preferred_element_type=F32)
               + jnp.dot(pn.astype(BF16), vn, preferred_element_type=F32))
        return acc / l

    o = one(q0) - lam * one(q1)
    o_ref[...] = _subln(o, sg_ref, lam_init).astype(BF16)


def _lam_specs(nargs):
    zero = lambda *a: (0, 0)
    return [pl.BlockSpec((1, HEAD_DIM), zero)] * 4 + [pl.BlockSpec((1, V_DIM), zero)]


def _lam_args(lq1, lk1, lq2, lk2, subln_g):
    return [a.astype(F32)[None, :] for a in (lq1, lk1, lq2, lk2, subln_g)]


def _attn_prompt(q3, kt3, v4, lam_args, lam_init, tq, g_heads):
    b, s, da = q3.shape
    nh = da // V_DIM
    assert s % tq == 0 and tq % CHUNK == 0 and nh % g_heads == 0
    gw = g_heads * V_DIM
    return pl.pallas_call(
        functools.partial(_attn_prompt_body, tq=tq, g_heads=g_heads, lam_init=lam_init),
        grid=(b, nh // g_heads, s // tq),
        in_specs=_lam_specs(3) + [
            pl.BlockSpec((None, tq, gw), lambda bi, h, i: (bi, i, h)),
            pl.BlockSpec((None, gw, s), lambda bi, h, i: (bi, h, 0)),
            pl.BlockSpec((None, g_heads, s, V_DIM), lambda bi, h, i: (bi, h, 0, 0))],
        out_specs=pl.BlockSpec((None, tq, gw), lambda bi, h, i: (bi, i, h)),
        out_shape=jax.ShapeDtypeStruct((b, s, da), BF16),
        compiler_params=_cparams("parallel", "parallel", "parallel"),
        name="attn_prompt",
    )(*lam_args, q3, kt3, v4)


def _attn_sample(q3, ckt3, cv4, kn3, vn3, lam_args, lam_init):
    b, s, da = q3.shape
    past = ckt3.shape[2]
    nh = da // V_DIM
    assert past % CHUNK == 0
    new = lambda bi, h: (bi, 0, h)
    return pl.pallas_call(
        functools.partial(_attn_sample_body, past=past, lam_init=lam_init),
        grid=(b, nh),
        in_specs=_lam_specs(2) + [
            pl.BlockSpec((None, s, V_DIM), new),
            pl.BlockSpec((None, V_DIM, past), lambda bi, h: (bi, h, 0)),
            pl.BlockSpec((None, None, past, V_DIM), lambda bi, h: (bi, h, 0, 0)),
            pl.BlockSpec((None, s, V_DIM), new), pl.BlockSpec((None, s, V_DIM), new)],
        out_specs=pl.BlockSpec((None, s, V_DIM), new),
        out_shape=jax.ShapeDtypeStruct((b, s, da), BF16),
        compiler_params=_cparams("parallel", "parallel"),
        name="attn_sample",
    )(*lam_args, q3, ckt3, cv4, kn3, vn3)


def _out_proj_body(po_ref, on_ref, x_ref, wo_ref, g2_ref, wpq_ref, h_ref, xnt_ref, qp_ref):
    dp = po_ref.shape[-1]
    mix = (jnp.dot(po_ref[...], wo_ref[0:dp, :], preferred_element_type=F32)
           + jnp.dot(on_ref[...], wo_ref[dp:, :], preferred_element_type=F32))
    h = x_ref[...] + mix
    h_ref[...] = h
    r = lax.rsqrt(jnp.mean(h * h, axis=-1, keepdims=True) + EPS)
    xn = h * r * g2_ref[...]
    xnt_ref[...] = xn.T.astype(BF16)
    qp_ref[...] = jnp.dot(xn.astype(BF16), wpq_ref[...], preferred_element_type=F32).astype(BF16)


def _out_proj(po2, on2, x2, w_out_b, g2, w_pq_b, tm):
    t_tok, d_model = x2.shape
    dp, da = po2.shape[1], on2.shape[1]
    dq = w_pq_b.shape[1]
    row = lambda i: (i, 0)
    const = lambda i: (0, 0)
    resident = functools.partial(pl.BlockSpec, index_map=const, pipeline_mode=pl.Buffered(1))
    return pl.pallas_call(
        _out_proj_body,
        grid=(t_tok // tm,),
        in_specs=[pl.BlockSpec((tm, dp), row), pl.BlockSpec((tm, da), row), pl.BlockSpec((tm, d_model), row),
                  resident((d_model, d_model)), pl.BlockSpec((1, d_model), const), resident((d_model, dq))],
        out_specs=[pl.BlockSpec((tm, d_model), row), pl.BlockSpec((d_model, tm), lambda i: (0, i)),
                   pl.BlockSpec((tm, dq), row)],
        out_shape=[jax.ShapeDtypeStruct((t_tok, d_model), F32),
                   jax.ShapeDtypeStruct((d_model, t_tok), BF16),
                   jax.ShapeDtypeStruct((t_tok, dq), BF16)],
        compiler_params=_cparams("parallel"),
        name="out_proj",
    )(po2, on2, x2, w_out_b, g2.astype(F32)[None, :], w_pq_b)


def _top_rows(s, n_take, exact):
    rows, cols = s.shape
    ridx = lax.broadcasted_iota(jnp.int32, (rows, cols), 0)
    tidx = lax.broadcasted_iota(jnp.int32, (n_take, cols), 0)
    cur, rank, srt = s, jnp.full((rows, cols), float(n_take), F32), jnp.zeros((n_take, cols), F32)
    for k in range(n_take):
        m = jnp.max(cur, axis=0, keepdims=True)
        if exact:
            first = jnp.min(jnp.where(cur == m, ridx, rows), axis=0, keepdims=True)
            sel = ridx == first
        else:
            sel = cur == m
        rank = jnp.where(sel, float(k), rank)
        cur = jnp.where(sel, -jnp.inf, cur)
        srt = jnp.where(tidx == k, m, srt)
    return rank, srt


_HALF_K = PEER_TOPK // 2


def _peer_tables(s1, s2, exact):
    rank1, srt1 = _top_rows(s1, PEER_TOPK, exact)
    rank2, srt2 = _top_rows(s2, PEER_TOPK, exact)
    cols = s1.shape[1]
    bidx = lax.broadcasted_iota(jnp.int32, (_HALF_K, cols), 0)
    pieces = [srt1[0:1, :] + srt2]
    for a in range(1, _HALF_K):
        pieces.append(jnp.where(bidx < PEER_TOPK // (a + 1), srt1[a:a + 1, :] + srt2[0:_HALF_K, :], -jnp.inf))
    pieces.append(srt1[_HALF_K:, :] + srt2[0:1, :])
    cand = jnp.concatenate(pieces, axis=0)
    crank, _ = _top_rows(cand, PEER_TOPK, exact)
    chosen = (crank < PEER_TOPK).astype(F32)
    best = srt1[0:1, :] + srt2[0:1, :]
    z = jnp.sum(chosen * jnp.exp(cand - best), axis=0, keepdims=True)
    lens = [jnp.sum(chosen[0:PEER_TOPK, :], axis=0, keepdims=True)]
    for a in range(1, _HALF_K):
        lo = PEER_TOPK + (a - 1) * _HALF_K
        lens.append(jnp.sum(chosen[lo:lo + _HALF_K, :], axis=0, keepdims=True))
    tail = PEER_TOPK + (_HALF_K - 1) * _HALF_K
    lens += [chosen[tail + a:tail + a + 1, :] for a in range(PEER_TOPK - _HALF_K)]
    lenrow = jnp.zeros(s1.shape, F32)
    for a in range(PEER_TOPK):
        lenrow = lenrow + jnp.where(rank1 == float(a), lens[a], 0.0)
    taken = (jnp.sum((rank1 < PEER_TOPK).astype(F32), axis=0, keepdims=True)
             + jnp.sum((rank2 < PEER_TOPK).astype(F32), axis=0, keepdims=True)
             + jnp.sum(chosen, axis=0, keepdims=True))
    c1 = jnp.exp(s1 - srt1[0:1, :]) / z
    e2 = jnp.exp(s2 - srt2[0:1, :])
    return lenrow, c1, rank2, e2, taken


def _peer_topk_body(qp_ref, sk_ref, len_ref, c1_ref, r2_ref, e2_ref):
    n_lc = r2_ref.shape[0]

    def emit(exact):
        worst = jnp.zeros((1, LANES), F32)
        for lc in range(n_lc):
            ls = slice(lc * LANES, (lc + 1) * LANES)
            qp = qp_ref[ls, :]
            s1 = _nt_dot(sk_ref[0], qp[:, :N_KEYS])
            s2 = _nt_dot(sk_ref[1], qp[:, N_KEYS:])
            lenrow, c1, rank2, e2, taken = _peer_tables(s1, s2, exact)
            len_ref[:, ls] = lenrow
            c1_ref[:, ls] = c1
            r2_ref[lc] = rank2
            e2_ref[lc] = e2
            worst = jnp.maximum(worst, jnp.abs(taken - 3.0 * PEER_TOPK))
        return jnp.max(worst)

    tie = emit(False)

    @pl.when(tie > 0.0)
    def _():
        emit(True)


def _peer_topk(qp2, sub_keys_b, tme):
    t_tok, dq = qp2.shape
    assert dq == PEER_HEADS * 2 * N_KEYS and t_tok % tme == 0 and tme % LANES == 0
    n_lc = tme // LANES
    row_out = jax.ShapeDtypeStruct((PEER_HEADS, N_KEYS, t_tok), F32)
    tile_out = jax.ShapeDtypeStruct((PEER_HEADS, t_tok // LANES, N_KEYS, LANES), F32)
    row_blk = pl.BlockSpec((None, N_KEYS, tme), lambda i, h: (h, 0, i))
    tile_blk = pl.BlockSpec((None, n_lc, N_KEYS, LANES), lambda i, h: (h, i, 0, 0))
    return pl.pallas_call(
        _peer_topk_body,
        grid=(t_tok // tme, PEER_HEADS),
        in_specs=[pl.BlockSpec((tme, 2 * N_KEYS), lambda i, h: (i, h)),
                  pl.BlockSpec(sub_keys_b.shape, lambda i, h: (0, 0, 0))],
        out_specs=[row_blk, row_blk, tile_blk, tile_blk],
        out_shape=[row_out, row_out, tile_out, tile_out],
        compiler_params=_cparams("parallel", "parallel"),
        name="peer_topk",
    )(qp2, sub_keys_b)


def _peer_main_body(xnt_ref, u_ref, vt_ref, len_ref, c1_ref, r2_ref, e2_ref, out_ref, ht_ref, w_ref, acc_ref,
                    *, a_per, n_chunks):
    g = pl.program_id(0)
    slot = g % 2

    @pl.when(g == 0)
    def _():
        w_ref[1] = jnp.zeros(w_ref.shape[1:], BF16)
        acc_ref[...] = jnp.zeros_like(acc_ref)

    n_lc = xnt_ref.shape[-1] // LANES
    ht_ref[...] = jnp.dot(u_ref[...], xnt_ref[...], preferred_element_type=F32)

    prev = jnp.dot(vt_ref[...], w_ref[1 - slot], preferred_element_type=F32)
    first_chunk = (jnp.maximum(g - 1, 0) % n_chunks) == 0
    acc = jnp.where(first_chunk, prev, acc_ref[...] + prev)
    acc_ref[...] = acc
    out_ref[...] = acc

    for lc in range(n_lc):
        ls = slice(lc * LANES, (lc + 1) * LANES)
        for al in range(a_per):
            rows = slice(al * N_KEYS, (al + 1) * N_KEYS)
            gate = jnp.zeros((N_KEYS, LANES), F32)
            for h in range(PEER_HEADS):
                ln = len_ref[h, al:al + 1, ls]
                cc = c1_ref[h, al:al + 1, ls]
                gate = gate + jnp.where(r2_ref[h, lc] < ln, e2_ref[h, lc], 0.0) * cc
            act = jax.nn.gelu(ht_ref[rows, ls])
            w_ref[slot, rows, ls] = (gate * act).astype(BF16)


def _peer_main(xnt, u_b, vt_b, tabs, tm, ec):
    d_model, t_tok = xnt.shape
    n_exp = u_b.shape[0]
    assert n_exp == N_KEYS * N_KEYS and n_exp % ec == 0 and ec % N_KEYS == 0 and t_tok % tm == 0
    a_per = ec // N_KEYS
    n_chunks = n_exp // ec
    n_steps = (t_tok // tm) * n_chunks
    cur = lambda g: jnp.minimum(g, n_steps - 1)
    prv = lambda g: jnp.maximum(g - 1, 0)
    row_tab = pl.BlockSpec((PEER_HEADS, a_per, tm), lambda g: (0, cur(g) % n_chunks, cur(g) // n_chunks))
    tile_tab = pl.BlockSpec((PEER_HEADS, tm // LANES, N_KEYS, LANES), lambda g: (0, cur(g) // n_chunks, 0, 0))
    return pl.pallas_call(
        functools.partial(_peer_main_body, a_per=a_per, n_chunks=n_chunks),
        grid=(n_steps + 1,),
        in_specs=[pl.BlockSpec((d_model, tm), lambda g: (0, cur(g) // n_chunks)),
                  pl.BlockSpec((ec, d_model), lambda g: (cur(g) % n_chunks, 0)),
                  pl.BlockSpec((d_model, ec), lambda g: (0, prv(g) % n_chunks)),
                  row_tab, row_tab, tile_tab, tile_tab],
        out_specs=pl.BlockSpec((d_model, tm), lambda g: (0, prv(g) // n_chunks)),
        out_shape=jax.ShapeDtypeStruct((d_model, t_tok), F32),
        scratch_shapes=[pltpu.VMEM((ec, tm), F32), pltpu.VMEM((2, ec, tm), BF16), pltpu.VMEM((d_model, tm), F32)],
        compiler_params=_cparams("arbitrary"),
        name="peer_main",
    )(xnt, u_b, vt_b, *tabs)


def _finish_body(h_ref, pt_ref, y_ref):
    y_ref[...] = h_ref[...] + pt_ref[...].T


def _finish(h2, peer_t, tm):
    t_tok, d_model = h2.shape
    return pl.pallas_call(
        _finish_body,
        grid=(t_tok // tm,),
        in_specs=[pl.BlockSpec((tm, d_model), lambda i: (i, 0)), pl.BlockSpec((d_model, tm), lambda i: (0, i))],
        out_specs=pl.BlockSpec((tm, d_model), lambda i: (i, 0)),
        out_shape=jax.ShapeDtypeStruct((t_tok, d_model), F32),
        compiler_params=_cparams("parallel"),
        name="finish",
    )(h2, peer_t)


def _layer(x, k_hist, v_hist, pool_hist, w, lam_init):
    b, s, d_model = x.shape
    d_pool = w["pool_scale"].shape[0]
    d_attn = d_model - d_pool
    n_heads = d_attn // V_DIM
    t_tok = b * s
    past = 0 if k_hist is None else k_hist.shape[1]
    tm = 512 if t_tok % 512 == 0 else LANES
    assert t_tok % tm == 0

    x2 = x.reshape(t_tok, d_model)
    u, q, kf, kb, vf, vb = _in_proj(x2, s, past, w["norm1_g"], w["w_in"], w["q_norm_g"], w["k_norm_g"],
                                    d_pool, d_attn, tm)
    po = _pool(u.reshape(b, s, d_pool), pool_hist, past, w["w_pool"], w["pool_scale"])
    lam_args = _lam_args(w["lambda_q1"], w["lambda_k1"], w["lambda_q2"], w["lambda_k2"], w["subln_g"])
    q3 = q.reshape(b, s, d_attn)
    if k_hist is None:
        on = _attn_prompt(q3, kb, vb, lam_args, lam_init, tq=256, g_heads=4)
        k_out = jnp.transpose(kf.reshape(b, n_heads, 2, HEAD_DIM, s), (0, 4, 1, 2, 3))
        v_out = jnp.transpose(vf, (0, 2, 1, 3))
    else:
        ckt = jnp.transpose(k_hist, (0, 2, 3, 4, 1)).reshape(b, d_attn, past)
        cv4 = jnp.transpose(v_hist, (0, 2, 1, 3))
        on = _attn_sample(q3, ckt, cv4, kb.reshape(b, s, d_attn), vb.reshape(b, s, d_attn), lam_args, lam_init)
        k_out = kf.reshape(b, s, n_heads, 2, HEAD_DIM)
        v_out = vf.reshape(b, s, n_heads, V_DIM)
    h, xnt, qp = _out_proj(po.reshape(t_tok, d_pool), on.reshape(t_tok, d_attn), x2, w["w_out"], w["norm2_g"],
                           w["w_pq"], tm)
    tabs = _peer_topk(qp, w["sub_keys"], min(2 * LANES, t_tok))
    peer_t = _peer_main(xnt, w["u_tab"], w["v_tab_t"], tabs, tm, ec=1024)
    y = _finish(h, peer_t, tm)
    pool_new = u.reshape(b, s, d_pool)[:, s - POOL_HIST:, :]
    return y.reshape(b, s, d_model), k_out, v_out, pool_new


def kernel(x_prompt, x_sample, cache_k, cache_v, state_pool, norm1_g, w_in, w_pool, pool_scale, q_norm_g, k_norm_g,
           lambda_q1, lambda_k1, lambda_q2, lambda_k2, subln_g, w_out, norm2_g, w_pq, sub_keys, u_tab, v_tab):
    depth = w_in.shape[0]
    bp = x_prompt.shape[0]
    d_pool = pool_scale.shape[-1]
    yp, ys = x_prompt, x_sample
    outs = [[] for _ in range(6)]
    for l in range(depth):
        lam_init = 0.8 - 0.6 * math.exp(-0.3 * (l + 1))
        w = dict(norm1_g=norm1_g[l], w_in=w_in[l].astype(BF16), w_pool=w_pool[l].astype(BF16),
                 pool_scale=pool_scale[l], q_norm_g=q_norm_g[l], k_norm_g=k_norm_g[l],
                 lambda_q1=lambda_q1[l], lambda_k1=lambda_k1[l], lambda_q2=lambda_q2[l], lambda_k2=lambda_k2[l],
                 subln_g=subln_g[l], w_out=w_out[l].astype(BF16), norm2_g=norm2_g[l], w_pq=w_pq[l].astype(BF16),
                 sub_keys=sub_keys[l].astype(BF16), u_tab=u_tab[l].astype(BF16),
                 v_tab_t=v_tab[l].T.astype(BF16))
        zero_hist = jnp.zeros((bp, POOL_HIST, d_pool), x_prompt.dtype)
        yp, kp, vp, pp = _layer(yp, None, None, zero_hist, w, lam_init)
        ys, ks, vs, ps = _layer(ys, cache_k[l], cache_v[l], state_pool[l], w, lam_init)
        for lst, val in zip(outs, (kp, vp, pp, ks, vs, ps)):
            lst.append(val)
    return (yp, ys) + tuple(jnp.stack(o) for o in outs)
```
